```python
import math
import jax, jax.numpy as jnp
from jax import lax
import numpy as np

D_MODEL = 1024
BATCH = 16
SEQ = 2048
DEPTH = 4
DEC_BATCH = 16
DEC_SEQ = 4096
PAST_LEN = 128

N_MEM = 256
CHUNK = 64
RMS_EPS = 1e-6
ROPE_BASE = 10000.0
RET_HEADS = 4
RET_DK = 128
RET_DV = 128
HG_HEADS = 4
HG_DK = 128
HG_DV = 128
GDN_HEADS = 8
GDN_DK = 128
GDN_DV = 128
CONV_W = 5
XA_HEADS = 4
XA_DH = D_MODEL // XA_HEADS
D_FF = int(math.ceil(8 * D_MODEL / 3 / 256)) * 256

N_EVEN = (DEPTH + 1) // 2
N_ODD = DEPTH // 2
RET_QK = RET_HEADS * RET_DK
RET_V = RET_HEADS * RET_DV
HG_K = HG_HEADS * HG_DK
HG_V = HG_HEADS * HG_DV
EVEN_IN = 2 * RET_QK + 2 * RET_V + 3 * HG_K + 2 * HG_V
EVEN_MIX = RET_V + HG_V
GDN_QK = GDN_HEADS * GDN_DK
GDN_V = GDN_HEADS * GDN_DV
GDN_CONV_CH = 2 * GDN_QK + GDN_V
ODD_IN = 2 * GDN_QK + 2 * GDN_V + 4 * GDN_HEADS

kernel_name = "hybrid_bidir_retention_hgrn2_gdn_encoder"

F32 = jnp.float32


def rmsnorm(x, g):
    xf = x.astype(F32)
    r = lax.rsqrt(jnp.mean(xf * xf, axis=-1, keepdims=True) + RMS_EPS)
    return (xf * r).astype(x.dtype) * g


def head_rmsnorm(x, g):
    xf = x.astype(F32)
    return xf * lax.rsqrt(jnp.mean(xf * xf, axis=-1, keepdims=True) + RMS_EPS) * g


def head_layernorm(x, g):
    xf = x.astype(F32)
    mu = jnp.mean(xf, axis=-1, keepdims=True)
    xc = xf - mu
    return xc * lax.rsqrt(jnp.mean(xc * xc, axis=-1, keepdims=True) + RMS_EPS) * g


def l2norm(x):
    return x * lax.rsqrt(jnp.sum(x * x, axis=-1, keepdims=True) + RMS_EPS)


def flip(t):
    return jnp.flip(t, axis=1)


def rope_tables(L, d):
    inv = 1.0 / (ROPE_BASE ** (jnp.arange(d // 2, dtype=F32) / (d // 2)))
    ang = jnp.arange(L, dtype=F32)[:, None] * inv[None, :]
    return jnp.cos(ang), jnp.sin(ang)


def apply_rope(x, cos, sin):
    x1, x2 = jnp.split(x, 2, axis=-1)
    c = cos[None, :, None, :]
    s = sin[None, :, None, :]
    return jnp.concatenate([x1 * c - x2 * s, x1 * s + x2 * c], axis=-1)


def to_chunks(x):
    B, L, H = x.shape[:3]
    x = x.reshape((B, L // CHUNK, CHUNK, H) + x.shape[3:])
    return jnp.moveaxis(x, (1, 3), (0, 2))


def from_chunks(x):
    x = jnp.moveaxis(x, (0, 2), (1, 3))
    B, N, C, H = x.shape[:4]
    return x.reshape((B, N * C, H) + x.shape[4:])


def chunk_gla(q, k, v, log_f):
    B, L, H, dk = q.shape
    dv = v.shape[-1]
    scalar = log_f.shape[-1] == 1
    qc, kc, vc = to_chunks(q), to_chunks(k), to_chunks(v)
    bc = jnp.cumsum(to_chunks(log_f), axis=3)
    causal = jnp.tril(jnp.ones((CHUNK, CHUNK), dtype=bool))

    def step(S, xs):
        q_, k_, v_, b_ = xs
        b_last = b_[:, :, -1:, :]
        o = jnp.einsum('bhck,bhkv->bhcv', q_ * jnp.exp(b_), S)
        diff = b_[:, :, :, None, :] - b_[:, :, None, :, :]
        decay = jnp.exp(jnp.where(causal[:, :, None], diff, -jnp.inf))
        if scalar:
            a = jnp.einsum('bhik,bhjk->bhij', q_, k_) * decay[..., 0]
        else:
            a = jnp.einsum('bhik,bhjk,bhijk->bhij', q_, k_, decay)
        o = o + jnp.einsum('bhij,bhjv->bhiv', a, v_)
        S = jnp.exp(b_last[:, :, 0, :, None]) * S + jnp.einsum(
            'bhck,bhcv->bhkv', k_ * jnp.exp(b_last - b_), v_)
        return S, o

    S0 = jnp.zeros((B, H, dk, dv), q.dtype)
    _, o = lax.scan(step, S0, (qc, kc, vc, bc))
    return from_chunks(o)


def chunk_gated_delta(q, k, v, log_a, beta):
    B, L, H, dk = q.shape
    dv = v.shape[-1]
    qc, kc, vc = to_chunks(q), to_chunks(k), to_chunks(v)
    g = jnp.cumsum(to_chunks(log_a), axis=-1)
    bt = to_chunks(beta)[..., None]
    incl = jnp.tril(jnp.ones((CHUNK, CHUNK), dtype=bool))
    strict = jnp.tril(jnp.ones((CHUNK, CHUNK), dtype=bool), -1)
    decay = jnp.exp(jnp.where(incl, g[..., :, None] - g[..., None, :], -jnp.inf))
    kb = kc * bt
    a = jnp.where(strict, jnp.einsum('...ik,...jk->...ij', kb, kc) * decay, 0.0)
    eye = jnp.eye(CHUNK, dtype=q.dtype)
    t = lax.linalg.triangular_solve(a + eye, jnp.broadcast_to(eye, a.shape),
                                    left_side=True, lower=True)
    u = jnp.einsum('...ij,...jv->...iv', t, vc * bt)
    w = jnp.einsum('...ij,...jk->...ik', t, kb * jnp.exp(g)[..., None])
    qk = jnp.where(incl, jnp.einsum('...ik,...jk->...ij', qc, kc) * decay, 0.0)

    def step(S, xs):
        q_, k_, u_, w_, qk_, g_ = xs
        g_last = g_[..., -1:]
        v_new = u_ - jnp.einsum('bhck,bhkv->bhcv', w_, S)
        o = jnp.einsum('bhck,bhkv->bhcv', q_ * jnp.exp(g_)[..., None], S) + \
            jnp.einsum('bhij,bhjv->bhiv', qk_, v_new)
        S = jnp.exp(g_last)[..., None] * S + jnp.einsum(
            'bhck,bhcv->bhkv', k_ * jnp.exp(g_last - g_)[..., None], v_new)
        return S, o

    S0 = jnp.zeros((B, H, dk, dv), q.dtype)
    _, o = lax.scan(step, S0, (qc, kc, u, w, qk, g))
    return from_chunks(o)


def centred_dwconv(x, w):
    return lax.conv_general_dilated(
        x, w[:, None, :], window_strides=(1,), padding=[(CONV_W // 2, CONV_W // 2)],
        dimension_numbers=('NWC', 'WIO', 'NWC'), feature_group_count=x.shape[-1])


def even_mixer(h, w_in, w_out, lb, ret_g, hg_g, cos, sin):
    B, L, _ = h.shape
    p = (h @ w_in).astype(F32)
    sizes = [RET_QK, RET_QK, RET_V, RET_V, HG_K, HG_K, HG_K, HG_V, HG_V]
    idx = [int(s) for s in np.cumsum(sizes)[:-1]]
    rq, rk, rv, rg, hq, hff, hfb, hi, hgt = jnp.split(p, idx, axis=-1)

    q = apply_rope(rq.reshape(B, L, RET_HEADS, RET_DK), cos, sin)
    k = apply_rope(rk.reshape(B, L, RET_HEADS, RET_DK), cos, sin) * (RET_DK ** -0.5)
    v = rv.reshape(B, L, RET_HEADS, RET_DV)
    log_gamma = jnp.log1p(-jnp.exp2(-5.0 - jnp.arange(RET_HEADS, dtype=F32)))
    lg = jnp.broadcast_to(log_gamma[:, None], (B, L, RET_HEADS, 1))
    ret = chunk_gla(q, k, v, lg) + flip(chunk_gla(flip(q), flip(k), flip(v), lg))
    ret = head_layernorm(ret, ret_g) * jax.nn.silu(rg.reshape(B, L, RET_HEADS, RET_DV))

    lbh = lb.astype(F32).reshape(HG_HEADS, HG_DK)

    def gate(z):
        z = z.reshape(B, L, HG_HEADS, HG_DK)
        log_f = jnp.logaddexp(jnp.log(lbh), jnp.log1p(-lbh) + jax.nn.log_sigmoid(z))
        kk = (1.0 - lbh) * jax.nn.sigmoid(-z)
        return log_f, kk

    hqh = hq.reshape(B, L, HG_HEADS, HG_DK)
    hih = hi.reshape(B, L, HG_HEADS, HG_DV)
    lf_f, k_f = gate(hff)
    lf_b, k_b = gate(hfb)
    hg = chunk_gla(hqh, k_f, hih, lf_f) + \
        flip(chunk_gla(flip(hqh), flip(k_b), flip(hih), flip(lf_b)))
    hg = head_rmsnorm(hg, hg_g) * jax.nn.silu(hgt.reshape(B, L, HG_HEADS, HG_DV))

    o = jnp.concatenate([ret.reshape(B, L, RET_V), hg.reshape(B, L, HG_V)], axis=-1)
    return o.astype(h.dtype) @ w_out


def odd_mixer(h, w_in, conv_w, a_log, dt_bias, norm_g, w_out):
    B, L, _ = h.shape
    p = (h @ w_in).astype(F32)
    qkv, g, a, b = jnp.split(p, [GDN_CONV_CH, GDN_CONV_CH + GDN_V,
                                 GDN_CONV_CH + GDN_V + 2 * GDN_HEADS], axis=-1)
    qkv = jax.nn.silu(centred_dwconv(qkv, conv_w.astype(F32)))
    q, k, v = jnp.split(qkv, [GDN_QK, 2 * GDN_QK], axis=-1)
    q = l2norm(q.reshape(B, L, GDN_HEADS, GDN_DK)) * (GDN_DK ** -0.5)
    k = l2norm(k.reshape(B, L, GDN_HEADS, GDN_DK))
    v = v.reshape(B, L, GDN_HEADS, GDN_DV)
    a = a.reshape(B, L, 2, GDN_HEADS)
    b = b.reshape(B, L, 2, GDN_HEADS)
    log_a = -jnp.exp(a_log.astype(F32)) * jax.nn.softplus(a + dt_bias.astype(F32))
    beta = jax.nn.sigmoid(b)
    o = chunk_gated_delta(q, k, v, log_a[:, :, 0], beta[:, :, 0]) + \
        flip(chunk_gated_delta(flip(q), flip(k), flip(v), flip(log_a[:, :, 1]), flip(beta[:, :, 1])))
    o = head_rmsnorm(o, norm_g) * jax.nn.silu(g.reshape(B, L, GDN_HEADS, GDN_DV))
    return o.reshape(B, L, GDN_V).astype(h.dtype) @ w_out


def cross_attn(h, m, w_q, w_kv, w_o):
    B, L, _ = h.shape
    M = m.shape[1]
    q = (h @ w_q).reshape(B, L, XA_HEADS, XA_DH)
    k, v = jnp.split(m @ w_kv, 2, axis=-1)
    k = k.reshape(B, M, XA_HEADS, XA_DH)
    v = v.reshape(B, M, XA_HEADS, XA_DH)
    s = jnp.einsum('blhd,bmhd->bhlm', q, k).astype(F32) * (XA_DH ** -0.5)
    pr = jax.nn.softmax(s, axis=-1).astype(v.dtype)
    o = jnp.einsum('bhlm,bmhd->blhd', pr, v).reshape(B, L, D_MODEL)
    return o @ w_o


def swiglu(h, w_gu, w_down):
    gt, up = jnp.split(h @ w_gu, 2, axis=-1)
    return (jax.nn.silu(gt) * up) @ w_down


def trunk(x, mem, norm_mix, norm_xq, norm_mem, norm_ffn, norm_final,
          even_w_in, even_w_out, hgrn_lb_logits, ret_norm, hgrn_norm,
          gdn_w_in, gdn_conv, gdn_a_log, gdn_dt_bias, gdn_norm, gdn_w_out,
          xa_w_q, xa_w_kv, xa_w_o, ffn_w_gu, ffn_w_down):
    L = x.shape[1]
    cos, sin = rope_tables(L, RET_DK)
    lb_all = jnp.cumsum(jax.nn.softmax(hgrn_lb_logits.astype(F32), axis=0), axis=0)
    lb_all = lb_all - lb_all[:1]
    for i in range(DEPTH):
        j = i // 2
        h = rmsnorm(x, norm_mix[i])
        if i % 2 == 0:
            x = x + even_mixer(h, even_w_in[j], even_w_out[j], lb_all[j], ret_norm[j],
                               hgrn_norm[j], cos, sin)
        else:
            x = x + odd_mixer(h, gdn_w_in[j], gdn_conv[j], gdn_a_log[j], gdn_dt_bias[j],
                              gdn_norm[j], gdn_w_out[j])
        x = x + cross_attn(rmsnorm(x, norm_xq[i]), rmsnorm(mem, norm_mem[i]),
                           xa_w_q[i], xa_w_kv[i], xa_w_o[i])
        x = x + swiglu(rmsnorm(x, norm_ffn[i]), ffn_w_gu[i], ffn_w_down[i])
    return rmsnorm(x, norm_final)


def setup_inputs(seed: int = 0) -> dict:
    key = jax.random.key(seed)
    ks = jax.random.split(key, 32)
    nrm = lambda k, shape, scale: jax.random.normal(k, shape, F32) * scale
    gain = lambda k, shape: 1.0 + 0.02 * jax.random.normal(k, shape, F32)
    u_dt = jax.random.uniform(ks[13], (N_ODD, 2, GDN_HEADS), F32)
    dt = jnp.exp(u_dt * (math.log(0.1) - math.log(0.001)) + math.log(0.001))
    return {
        "x_prompt": nrm(ks[0], (BATCH, SEQ, D_MODEL), 1.0),
        "x_sample": nrm(ks[1], (DEC_BATCH, DEC_SEQ, D_MODEL), 1.0),
        "mem_prompt": nrm(ks[2], (BATCH, N_MEM, D_MODEL), 1.0),
        "mem_sample": nrm(ks[3], (DEC_BATCH, N_MEM, D_MODEL), 1.0),
        "norm_mix": gain(ks[4], (DEPTH, D_MODEL)),
        "norm_xq": gain(ks[5], (DEPTH, D_MODEL)),
        "norm_mem": gain(ks[6], (DEPTH, D_MODEL)),
        "norm_ffn": gain(ks[7], (DEPTH, D_MODEL)),
        "norm_final": gain(ks[8], (D_MODEL,)),
        "even_w_in": nrm(ks[9], (N_EVEN, D_MODEL, EVEN_IN), D_MODEL ** -0.5),
        "even_w_out": nrm(ks[10], (N_EVEN, EVEN_MIX, D_MODEL), EVEN_MIX ** -0.5),
        "hgrn_lb_logits": nrm(ks[11], (N_EVEN, HG_K), 0.5),
        "ret_norm": gain(ks[12], (N_EVEN, RET_HEADS, RET_DV)),
        "hgrn_norm": gain(ks[14], (N_EVEN, HG_HEADS, HG_DV)),
        "gdn_w_in": nrm(ks[15], (N_ODD, D_MODEL, ODD_IN), D_MODEL ** -0.5),
        "gdn_conv": nrm(ks[16], (N_ODD, CONV_W, GDN_CONV_CH), CONV_W ** -0.5),
        "gdn_a_log": jnp.log(jax.random.uniform(ks[17], (N_ODD, 2, GDN_HEADS), F32, 1.0, 16.0)),
        "gdn_dt_bias": dt + jnp.log(-jnp.expm1(-dt)),
        "gdn_norm": gain(ks[18], (N_ODD, GDN_DV)),
        "gdn_w_out": nrm(ks[19], (N_ODD, GDN_V, D_MODEL), GDN_V ** -0.5),
        "xa_w_q": nrm(ks[20], (DEPTH, D_MODEL, D_MODEL), D_MODEL ** -0.5),
        "xa_w_kv": nrm(ks[21], (DEPTH, D_MODEL, 2 * D_MODEL), D_MODEL ** -0.5),
        "xa_w_o": nrm(ks[22], (DEPTH, D_MODEL, D_MODEL), D_MODEL ** -0.5),
        "ffn_w_gu": nrm(ks[23], (DEPTH, D_MODEL, 2 * D_FF), D_MODEL ** -0.5),
        "ffn_w_down": nrm(ks[24], (DEPTH, D_FF, D_MODEL), D_FF ** -0.5),
    }


def reference(x_prompt, x_sample, mem_prompt, mem_sample, norm_mix, norm_xq, norm_mem,
              norm_ffn, norm_final, even_w_in, even_w_out, hgrn_lb_logits, ret_norm,
              hgrn_norm, gdn_w_in, gdn_conv, gdn_a_log, gdn_dt_bias, gdn_norm, gdn_w_out,
              xa_w_q, xa_w_kv, xa_w_o, ffn_w_gu, ffn_w_down):
    y_prompt = trunk(x_prompt, mem_prompt, norm_mix, norm_xq, norm_mem, norm_ffn, norm_final,
                     even_w_in, even_w_out, hgrn_lb_logits, ret_norm, hgrn_norm,
                     gdn_w_in, gdn_conv, gdn_a_log, gdn_dt_bias, gdn_norm, gdn_w_out,
                     xa_w_q, xa_w_kv, xa_w_o, ffn_w_gu, ffn_w_down)
    y_sample = trunk(x_sample, mem_sample, norm_mix, norm_xq, norm_mem, norm_ffn, norm_final,
                     even_w_in, even_w_out, hgrn_lb_logits, ret_norm, hgrn_norm,
                     gdn_w_in, gdn_conv, gdn_a_log, gdn_dt_bias, gdn_norm, gdn_w_out,
                     xa_w_q, xa_w_kv, xa_w_o, ffn_w_gu, ffn_w_down)
    return (y_prompt, y_sample)
```

```python
import functools
import math

import numpy as np
import jax
import jax.numpy as jnp
from jax import lax
from jax.experimental import pallas as pl
from jax.experimental.pallas import tpu as pltpu

F32 = jnp.float32
BF16 = jnp.bfloat16

D_MODEL = 1024
DEPTH = 4
RMS_EPS = 1e-6
ROPE_BASE = 10000.0
HEAD_DIM = 128
RET_HEADS = 4
HG_HEADS = 4
GDN_HEADS = 8
CONV_W = 5
XA_HEADS = 4
XA_DH = D_MODEL // XA_HEADS
D_FF = 2816
CHUNK = 64
NEUMANN_BLOCK = 16

VMEM_LIMIT_BYTES = 56 * 1024 * 1024


def _cparams(sem):
    return pltpu.CompilerParams(dimension_semantics=sem, vmem_limit_bytes=VMEM_LIMIT_BYTES)


def _rms(x, g):
    r = lax.rsqrt(jnp.mean(x * x, axis=-1, keepdims=True) + RMS_EPS)
    return x * r * g


def _silu(x):
    return x * (1.0 / (1.0 + jnp.exp(-x)))


def _dot(a, b):
    return jnp.dot(a.astype(BF16), b.astype(BF16), preferred_element_type=F32)


def _dot_nt(a, b):
    return lax.dot_general(a.astype(BF16), b.astype(BF16), (((1,), (1,)), ((), ())),
                           preferred_element_type=F32)


def _dot_tn(a, b):
    return lax.dot_general(a.astype(BF16), b.astype(BF16), (((0,), (0,)), ((), ())),
                           preferred_element_type=F32)


def _split3(x):
    hi = x.astype(BF16)
    r1 = x - hi.astype(F32)
    mid = r1.astype(BF16)
    lo = (r1 - mid.astype(F32)).astype(BF16)
    return hi, mid, lo


def _iota2(n, m):
    return (lax.broadcasted_iota(jnp.int32, (n, m), 0), lax.broadcasted_iota(jnp.int32, (n, m), 1))


def _norm_matmul_kernel(x_ref, g_ref, w_ref, o_ref, h_scr):
    @pl.when(pl.program_id(1) == 0)
    def _():
        h_scr[...] = _rms(x_ref[...], g_ref[...]).astype(BF16)
    o_ref[...] = jnp.dot(h_scr[...], w_ref[...], preferred_element_type=F32).astype(o_ref.dtype)


def norm_matmul(x, g, w, out_dtype, tm=1024, tn=512):
    t, d = x.shape
    n = w.shape[1]
    tm = min(tm, t)
    tn = min(tn, n)
    assert t % tm == 0 and n % tn == 0
    return pl.pallas_call(
        _norm_matmul_kernel,
        grid=(t // tm, n // tn),
        in_specs=[pl.BlockSpec((tm, d), lambda i, j: (i, 0)),
                  pl.BlockSpec((1, d), lambda i, j: (0, 0)),
                  pl.BlockSpec((d, tn), lambda i, j: (0, j))],
        out_specs=pl.BlockSpec((tm, tn), lambda i, j: (i, j)),
        out_shape=jax.ShapeDtypeStruct((t, n), out_dtype),
        scratch_shapes=[pltpu.VMEM((tm, d), BF16)],
        compiler_params=_cparams(("parallel", "arbitrary")),
        name="norm_matmul",
    )(x, g.reshape(1, d), w)


def _proj_residual_kernel(*refs, n_in):
    o_refs = refs[:n_in]
    w_refs = refs[n_in:2 * n_in]
    x_ref = refs[2 * n_in]
    out_ref = refs[2 * n_in + 1]
    acc = x_ref[...]
    for o_ref, w_ref in zip(o_refs, w_refs):
        acc = acc + jnp.dot(o_ref[...], w_ref[...], preferred_element_type=F32)
    out_ref[...] = acc


def proj_residual(os_, ws, x, tm=1024):
    t, d = x.shape
    tm = min(tm, t)
    assert t % tm == 0
    n_in = len(os_)
    in_specs = ([pl.BlockSpec((tm, o.shape[1]), lambda i: (i, 0)) for o in os_]
                + [pl.BlockSpec(w.shape, lambda i: (0, 0)) for w in ws]
                + [pl.BlockSpec((tm, d), lambda i: (i, 0))])
    return pl.pallas_call(
        functools.partial(_proj_residual_kernel, n_in=n_in),
        grid=(t // tm,),
        in_specs=in_specs,
        out_specs=pl.BlockSpec((tm, d), lambda i: (i, 0)),
        out_shape=jax.ShapeDtypeStruct((t, d), F32),
        compiler_params=_cparams(("parallel",)),
        name="proj_residual",
    )(*os_, *ws, x)


def _xattn_kernel(x_ref, g_ref, wq_ref, kv_ref, wo_ref, out_ref):
    x = x_ref[0]
    h = _rms(x, g_ref[...]).astype(BF16)
    q = jnp.dot(h, wq_ref[...], preferred_element_type=F32)
    heads = []
    for hd in range(XA_HEADS):
        qh = q[:, hd * XA_DH:(hd + 1) * XA_DH]
        kh = kv_ref[0, :, hd * XA_DH:(hd + 1) * XA_DH]
        vh = kv_ref[0, :, D_MODEL + hd * XA_DH:D_MODEL + (hd + 1) * XA_DH]
        s = _dot_nt(qh, kh) * (XA_DH ** -0.5)
        m = jnp.max(s, axis=-1, keepdims=True)
        e = jnp.exp(s - m)
        pr = e / jnp.sum(e, axis=-1, keepdims=True)
        heads.append(_dot(pr, vh).astype(BF16))
    o = jnp.concatenate(heads, axis=-1)
    out_ref[0] = x + jnp.dot(o, wo_ref[...], preferred_element_type=F32)


def xattn_residual(x, g, wq, kv, wo, tl=512):
    b, l, d = x.shape
    m = kv.shape[1]
    tl = min(tl, l)
    assert l % tl == 0
    return pl.pallas_call(
        _xattn_kernel,
        grid=(b, l // tl),
        in_specs=[pl.BlockSpec((1, tl, d), lambda i, j: (i, j, 0)),
                  pl.BlockSpec((1, d), lambda i, j: (0, 0)),
                  pl.BlockSpec((d, d), lambda i, j: (0, 0)),
                  pl.BlockSpec((1, m, 2 * d), lambda i, j: (i, 0, 0)),
                  pl.BlockSpec((d, d), lambda i, j: (0, 0))],
        out_specs=pl.BlockSpec((1, tl, d), lambda i, j: (i, j, 0)),
        out_shape=jax.ShapeDtypeStruct((b, l, d), F32),
        compiler_params=_cparams(("parallel", "parallel")),
        name="xattn_residual",
    )(x, g.reshape(1, d), wq, kv, wo)


def _ffn_kernel(x_ref, g_ref, wg_ref, wu_ref, wd_ref, gf_ref, out_ref, *, tf, final_norm):
    x = x_ref[...]
    h = _rms(x, g_ref[...]).astype(BF16)
    acc = x
    for j in range(D_FF // tf):
        gt = jnp.dot(h, wg_ref[:, j * tf:(j + 1) * tf], preferred_element_type=F32)
        up = jnp.dot(h, wu_ref[:, j * tf:(j + 1) * tf], preferred_element_type=F32)
        act = (_silu(gt) * up).astype(BF16)
        acc = acc + jnp.dot(act, wd_ref[j * tf:(j + 1) * tf, :], preferred_element_type=F32)
    if final_norm:
        acc = _rms(acc, gf_ref[...])
    out_ref[...] = acc


def ffn_residual(x, g, wg, wu, wd, gf, final_norm, tm=512, tf=256):
    t, d = x.shape
    tm = min(tm, t)
    assert t % tm == 0 and D_FF % tf == 0
    return pl.pallas_call(
        functools.partial(_ffn_kernel, tf=tf, final_norm=final_norm),
        grid=(t // tm,),
        in_specs=[pl.BlockSpec((tm, d), lambda i: (i, 0)),
                  pl.BlockSpec((1, d), lambda i: (0, 0)),
                  pl.BlockSpec((d, D_FF), lambda i: (0, 0)),
                  pl.BlockSpec((d, D_FF), lambda i: (0, 0)),
                  pl.BlockSpec((D_FF, d), lambda i: (0, 0)),
                  pl.BlockSpec((1, d), lambda i: (0, 0))],
        out_specs=pl.BlockSpec((tm, d), lambda i: (i, 0)),
        out_shape=jax.ShapeDtypeStruct((t, d), F32),
        compiler_params=_cparams(("parallel",)),
        name="ffn_residual",
    )(x, g.reshape(1, d), wg, wu, wd, gf.reshape(1, d))


def _retention_tables(c):
    lg = np.log1p(-np.exp2(-5.0 - np.arange(RET_HEADS, dtype=np.float64)))
    idx = np.arange(c, dtype=np.float64)
    dist = np.abs(idx[:, None] - idx[None, :])
    kscale = HEAD_DIM ** -0.5
    dmat = np.exp(lg[:, None, None] * dist[None]) * (1.0 + np.eye(c))[None] * kscale
    rows = np.stack([np.exp(lg[:, None] * (idx + 1.0)[None]),
                     np.exp(lg[:, None] * (c - 1.0 - idx)[None]) * kscale,
                     np.exp(lg[:, None] * (c - idx)[None]),
                     np.exp(lg[:, None] * idx[None]) * kscale], axis=1)
    rows = np.broadcast_to(rows[..., None], rows.shape + (HEAD_DIM,))
    gc = np.broadcast_to(np.exp(lg * c)[:, None, None], (RET_HEADS, 8, HEAD_DIM))
    return (jnp.asarray(dmat, F32), jnp.asarray(rows, F32), jnp.asarray(gc, F32))


def _rope(x, cos, sin):
    return x * cos + pltpu.roll(x, HEAD_DIM // 2, 1) * sin


def _retention_kernel(q_ref, k_ref, v_ref, gate_ref, cos_ref, sin_ref, dmat_ref, rows_ref, gc_ref,
                      gain_ref, out_ref, o_scr, *, n_chunks, c):
    dmat = dmat_ref[0]
    gc = gc_ref[0, 0:1, :]

    def load(i):
        sl = pl.ds(pl.multiple_of(i * c, c), c)
        q = _rope(q_ref[0, sl, :].astype(F32), cos_ref[sl, :], sin_ref[sl, :])
        k = _rope(k_ref[0, sl, :].astype(F32), cos_ref[sl, :], sin_ref[sl, :])
        return sl, q, k, v_ref[0, sl, :]

    def fwd(i, s):
        sl, q, k, v = load(i)
        a = _dot_nt(q, k) * dmat
        o_scr[sl, :] = _dot(a, v) + _dot(q * rows_ref[0, 0], s)
        return gc * s + _dot_tn(k * rows_ref[0, 1], v)

    lax.fori_loop(0, n_chunks, fwd, jnp.zeros((HEAD_DIM, HEAD_DIM), F32))

    def bwd(t, s):
        i = n_chunks - 1 - t
        sl, q, k, v = load(i)
        o = o_scr[sl, :] + _dot(q * rows_ref[0, 2], s)
        mu = jnp.mean(o, axis=-1, keepdims=True)
        oc = o - mu
        o = oc * lax.rsqrt(jnp.mean(oc * oc, axis=-1, keepdims=True) + RMS_EPS) * gain_ref[0]
        out_ref[0, sl, :] = (o * _silu(gate_ref[0, sl, :].astype(F32))).astype(out_ref.dtype)
        return gc * s + _dot_tn(k * rows_ref[0, 3], v)

    lax.fori_loop(0, n_chunks, bwd, jnp.zeros((HEAD_DIM, HEAD_DIM), F32))


def retention_heads(pb, cos, sin, gain, c=CHUNK):
    b, l, _ = pb.shape
    hh = RET_HEADS
    dmat, rows, gc = _retention_tables(c)
    col = lambda off: pl.BlockSpec((1, l, HEAD_DIM), lambda i, h: (i, 0, off + h))
    return pl.pallas_call(
        functools.partial(_retention_kernel, n_chunks=l // c, c=c),
        grid=(b, hh),
        in_specs=[col(0), col(hh), col(2 * hh), col(3 * hh),
                  pl.BlockSpec((l, HEAD_DIM), lambda i, h: (0, 0)),
                  pl.BlockSpec((l, HEAD_DIM), lambda i, h: (0, 0)),
                  pl.BlockSpec((1, c, c), lambda i, h: (h, 0, 0)),
                  pl.BlockSpec((1, 4, c, HEAD_DIM), lambda i, h: (h, 0, 0, 0)),
                  pl.BlockSpec((1, 8, HEAD_DIM), lambda i, h: (h, 0, 0)),
                  pl.BlockSpec((1, 1, HEAD_DIM), lambda i, h: (h, 0, 0))],
        out_specs=pl.BlockSpec((1, l, HEAD_DIM), lambda i, h: (i, 0, h)),
        out_shape=jax.ShapeDtypeStruct((b, l, hh * HEAD_DIM), BF16),
        scratch_shapes=[pltpu.VMEM((l, HEAD_DIM), F32)],
        compiler_params=_cparams(("parallel", "parallel")),
        name="retention_heads",
    )(pb, pb, pb, pb, cos, sin, dmat, rows, gc, gain.reshape(hh, 1, HEAD_DIM))


EXP_CLAMP = 80.0


def _hgrn_gate(z, log_lb, log1m_lb, one_m_lb):
    ls = jnp.minimum(z, 0.0) - jnp.log(1.0 + jnp.exp(-jnp.abs(z)))
    bb = log1m_lb + ls
    mx = jnp.maximum(log_lb, bb)
    lf = mx + jnp.log(1.0 + jnp.exp(-jnp.abs(log_lb - bb)))
    kk = one_m_lb * jnp.exp(ls - z)
    return lf, kk


def _hgrn_kernel(q_ref, v_ref, gate_ref, zf_ref, zb_ref, lb_ref, gain_ref, out_ref, o_scr,
                 *, n_chunks, c):
    row, colm = _iota2(c, c)
    tri_f = jnp.where(row >= colm, 1.0, 0.0).astype(BF16)
    tri_b = jnp.where(row <= colm, 1.0, 0.0).astype(BF16)
    log_lb = lb_ref[0, 0:1, :]
    log1m_lb = lb_ref[0, 1:2, :]
    one_m_lb = lb_ref[0, 2:3, :]

    def direction(i, st, z_ref, rev):
        sl = pl.ds(pl.multiple_of(i * c, c), c)
        q = q_ref[0, sl, :].astype(F32)
        v = v_ref[0, sl, :]
        lf, kk = _hgrn_gate(z_ref[0, sl, :], log_lb, log1m_lb, one_m_lb)
        hi, mid, lo = _split3(lf)
        tri = tri_b if rev else tri_f
        bcum = (jnp.dot(tri, hi, preferred_element_type=F32)
                + jnp.dot(tri, mid, preferred_element_type=F32)
                + jnp.dot(tri, lo, preferred_element_type=F32))
        mid_row = c // 2 if rev else c // 2 - 1
        end_row = 0 if rev else c - 1
        b_mid = bcum[mid_row:mid_row + 1, :]
        b_end = bcum[end_row:end_row + 1, :]
        e1 = jnp.exp(jnp.minimum(bcum - b_mid, EXP_CLAMP))
        e2 = jnp.exp(jnp.minimum(b_mid - bcum, EXP_CLAMP))
        a = _dot_nt(q * e1, kk * e2)
        a = jnp.where((row <= colm) if rev else (row >= colm), a, 0.0)
        o = _dot(a, v) + _dot_nt(q * (e1 * jnp.exp(b_mid)), st)
        st = jnp.exp(b_end) * st + _dot_tn(v, kk * (e2 * jnp.exp(b_end - b_mid)))
        return sl, o, st

    def fwd(i, st):
        sl, o, st = direction(i, st, zf_ref, False)
        o_scr[sl, :] = o
        return st

    lax.fori_loop(0, n_chunks, fwd, jnp.zeros((HEAD_DIM, HEAD_DIM), F32))

    def bwd(t, st):
        sl, o, st = direction(n_chunks - 1 - t, st, zb_ref, True)
        o = o_scr[sl, :] + o
        o = o * lax.rsqrt(jnp.mean(o * o, axis=-1, keepdims=True) + RMS_EPS) * gain_ref[0]
        out_ref[0, sl, :] = (o * _silu(gate_ref[0, sl, :].astype(F32))).astype(out_ref.dtype)
        return st

    lax.fori_loop(0, n_chunks, bwd, jnp.zeros((HEAD_DIM, HEAD_DIM), F32))


def hgrn_heads(pb, pf, lb_tab, gain, c=CHUNK):
    b, l, _ = pb.shape
    hh = HG_HEADS
    colb = lambda off: pl.BlockSpec((1, l, HEAD_DIM), lambda i, h: (i, 0, off + h))
    return pl.pallas_call(
        functools.partial(_hgrn_kernel, n_chunks=l // c, c=c),
        grid=(b, hh),
        in_specs=[colb(4 * hh), colb(5 * hh), colb(6 * hh), colb(0), colb(hh),
                  pl.BlockSpec((1, 8, HEAD_DIM), lambda i, h: (h, 0, 0)),
                  pl.BlockSpec((1, 1, HEAD_DIM), lambda i, h: (h, 0, 0))],
        out_specs=pl.BlockSpec((1, l, HEAD_DIM), lambda i, h: (i, 0, h)),
        out_shape=jax.ShapeDtypeStruct((b, l, hh * HEAD_DIM), BF16),
        scratch_shapes=[pltpu.VMEM((l, HEAD_DIM), F32)],
        compiler_params=_cparams(("parallel", "parallel")),
        name="hgrn_heads",
    )(pb, pb, pb, pf, pf, lb_tab, gain.reshape(hh, 1, HEAD_DIM))


def _unit_triangular_inverse(a, same_blk, same_2blk, c):
    row, colm = _iota2(c, c)
    eye = jnp.where(row == colm, 1.0, 0.0)
    p = jnp.where(same_blk, a, 0.0)
    x = eye - p
    n = 2
    while n < NEUMANN_BLOCK:
        p = _dot(p, p)
        x = x + _dot(x, p)
        n *= 2
    off1 = jnp.where(same_2blk & jnp.logical_not(same_blk), a, 0.0)
    x = x - _dot(_dot(x, off1), x)
    off2 = jnp.where(same_2blk, 0.0, a)
    x = x - _dot(_dot(x, off2), x)
    return x


def _gdn_kernel(q_ref, k_ref, v_ref, gate_ref, cwq_ref, cwk_ref, cwv_ref, ab_ref, par_ref, gain_ref,
                out_ref, xs_scr, qn_scr, kn_scr, vn_scr, o_scr, *, n_chunks, c, l):
    halo = 8
    row, colm = _iota2(c, c)
    eye_b = row == colm
    same_blk = (row // NEUMANN_BLOCK) == (colm // NEUMANN_BLOCK)
    same_2blk = (row // (2 * NEUMANN_BLOCK)) == (colm // (2 * NEUMANN_BLOCK))
    pre_u = jnp.where(row <= colm, 1.0, 0.0).astype(BF16)
    suf_u = jnp.where(row >= colm, 1.0, 0.0).astype(BF16)

    zeros_h = jnp.zeros((halo, HEAD_DIM), F32)

    def conv_pass(src_ref, cw_ref, dst_scr, mode):
        xs_scr[0:halo, :] = zeros_h
        xs_scr[halo + l:2 * halo + l, :] = zeros_h

        def cp(i, _):
            sl = pl.ds(pl.multiple_of(i * c, c), c)
            xs_scr[pl.ds(pl.multiple_of(halo + i * c, 8), c), :] = src_ref[0, sl, :].astype(F32)
            return 0
        lax.fori_loop(0, n_chunks, cp, 0)

        def cv(i, _):
            win = xs_scr[pl.ds(pl.multiple_of(i * c, 8), c + 2 * halo), :]
            acc = jnp.zeros((c, HEAD_DIM), F32)
            for w in range(CONV_W):
                off = halo + w - CONV_W // 2
                acc = acc + win[off:off + c, :] * cw_ref[0, w:w + 1, :]
            y = _silu(acc)
            if mode != "v":
                y = y * lax.rsqrt(jnp.sum(y * y, axis=-1, keepdims=True) + RMS_EPS)
            if mode == "q":
                y = y * (HEAD_DIM ** -0.5)
            dst_scr[pl.ds(pl.multiple_of(i * c, c), c), :] = y
            return 0
        lax.fori_loop(0, n_chunks, cv, 0)

    conv_pass(q_ref, cwq_ref, qn_scr, "q")
    conv_pass(k_ref, cwk_ref, kn_scr, "k")
    conv_pass(v_ref, cwv_ref, vn_scr, "v")

    def to_col(r):
        return jnp.sum(jnp.where(eye_b, r, 0.0), axis=1, keepdims=True)

    def chunk(i, s, rev):
        sl = pl.ds(pl.multiple_of(i * c, c), c)
        q = qn_scr[sl, :]
        k = kn_scr[sl, :]
        v = vn_scr[sl, :]
        raw = ab_ref[0, 0, i]
        la = -par_ref[0, 0, :, 0:c] * jax.nn.softplus(raw + par_ref[0, 1, :, 0:c])
        beta = 1.0 / (1.0 + jnp.exp(-raw))
        hi, mid, lo = _split3(la)
        um = suf_u if rev else pre_u
        gs = (jnp.dot(hi, um, preferred_element_type=F32)
              + jnp.dot(mid, um, preferred_element_type=F32)
              + jnp.dot(lo, um, preferred_element_type=F32))
        d = 1 if rev else 0
        g_row = gs[d:d + 1, :]
        beta_row = beta[2 + d:3 + d, :]
        g_end = g_row[:, 0:1] if rev else g_row[:, c - 1:c]
        g_col = to_col(g_row)
        beta_col = to_col(beta_row)
        incl = (row <= colm) if rev else (row >= colm)
        strict = (row < colm) if rev else (row > colm)
        decay = jnp.exp(jnp.where(incl, g_col - g_row, -jnp.inf))
        kb = k * beta_col
        a = jnp.where(strict, _dot_nt(kb, k) * decay, 0.0)
        t = _unit_triangular_inverse(a, same_blk, same_2blk, c)
        u = _dot(t, v * beta_col)
        w = _dot(t, kb * jnp.exp(g_col))
        qk = jnp.where(incl, _dot_nt(q, k) * decay, 0.0)
        v_new = u - _dot(w, s)
        o = _dot(q * jnp.exp(g_col), s) + _dot(qk, v_new)
        s = jnp.exp(g_end) * s + _dot_tn(k * jnp.exp(g_end - g_col), v_new)
        return sl, o, s

    def fwd(i, s):
        sl, o, s = chunk(i, s, False)
        o_scr[sl, :] = o
        return s

    lax.fori_loop(0, n_chunks, fwd, jnp.zeros((HEAD_DIM, HEAD_DIM), F32))

    def bwd(t, s):
        sl, o, s = chunk(n_chunks - 1 - t, s, True)
        o = o_scr[sl, :] + o
        o = o * lax.rsqrt(jnp.mean(o * o, axis=-1, keepdims=True) + RMS_EPS) * gain_ref[...]
        out_ref[0, sl, :] = (o * _silu(gate_ref[0, sl, :].astype(F32))).astype(out_ref.dtype)
        return s

    lax.fori_loop(0, n_chunks, bwd, jnp.zeros((HEAD_DIM, HEAD_DIM), F32))


def gdn_heads(pm, ab, conv_w, a_log, dt_bias, gain, c=CHUNK):
    b, l, _ = pm.shape
    hh = GDN_HEADS
    n = l // c
    assert c == 4 * NEUMANN_BLOCK
    abr = ab[:, :, :4 * hh].reshape(b, n, c, 4, hh)
    abr = jnp.transpose(abr, (0, 4, 1, 3, 2))
    abr = jnp.concatenate([abr, jnp.zeros_like(abr)], axis=3)
    pa = jnp.zeros((hh, 8), F32).at[:, 0:2].set(jnp.exp(a_log.astype(F32)).T)
    pd = jnp.zeros((hh, 8), F32).at[:, 0:2].set(dt_bias.astype(F32).T)
    par = jnp.broadcast_to(jnp.stack([pa, pd], axis=1)[..., None], (hh, 2, 8, HEAD_DIM))
    cw = conv_w.astype(F32).reshape(CONV_W, 3 * hh, HEAD_DIM).transpose(1, 0, 2)
    cw = jnp.concatenate([cw, jnp.zeros((3 * hh, 8 - CONV_W, HEAD_DIM), F32)], axis=1)
    col = lambda off: pl.BlockSpec((1, l, HEAD_DIM), lambda i, h: (i, 0, off + h))
    cws = lambda off: pl.BlockSpec((1, 8, HEAD_DIM), lambda i, h: (off + h, 0, 0))
    return pl.pallas_call(
        functools.partial(_gdn_kernel, n_chunks=n, c=c, l=l),
        grid=(b, hh),
        in_specs=[col(0), col(hh), col(2 * hh), col(3 * hh), cws(0), cws(hh), cws(2 * hh),
                  pl.BlockSpec((1, 1, n, 8, c), lambda i, h: (i, h, 0, 0, 0)),
                  pl.BlockSpec((1, 2, 8, HEAD_DIM), lambda i, h: (h, 0, 0, 0)),
                  pl.BlockSpec((1, HEAD_DIM), lambda i, h: (0, 0))],
        out_specs=pl.BlockSpec((1, l, HEAD_DIM), lambda i, h: (i, 0, h)),
        out_shape=jax.ShapeDtypeStruct((b, l, hh * HEAD_DIM), BF16),
        scratch_shapes=[pltpu.VMEM((l + 16, HEAD_DIM), F32),
                        pltpu.VMEM((l, HEAD_DIM), F32),
                        pltpu.VMEM((l, HEAD_DIM), F32),
                        pltpu.VMEM((l, HEAD_DIM), F32),
                        pltpu.VMEM((l, HEAD_DIM), F32)],
        compiler_params=_cparams(("parallel", "parallel")),
        name="gdn_heads",
    )(pm, pm, pm, pm, cw, cw, cw, abr, par, gain.reshape(1, HEAD_DIM))


def _rope_tables(l):
    half = HEAD_DIM // 2
    inv = 1.0 / (ROPE_BASE ** (jnp.arange(half, dtype=F32) / half))
    ang = jnp.arange(l, dtype=F32)[:, None] * inv[None, :]
    cos, sin = jnp.cos(ang), jnp.sin(ang)
    return jnp.concatenate([cos, cos], axis=-1), jnp.concatenate([-sin, sin], axis=-1)


def _even_weight_split(w_in):
    g = 512
    rq, rk, rv, rg, hq, hff, hfb, hi, hgt = [w_in[:, i * g:(i + 1) * g] for i in range(9)]
    return (jnp.concatenate([rq, rk, rv, rg, hq, hi, hgt], axis=1).astype(BF16),
            jnp.concatenate([hff, hfb], axis=1).astype(BF16))


def _trunk(x, mem, wts, lb_all):
    b, l, d = x.shape
    t = b * l
    m = mem.shape[1]
    cos, sin = _rope_tables(l)
    xf = x.reshape(t, d)
    memf = mem.reshape(b * m, d)
    for i in range(DEPTH):
        j = i // 2
        if i % 2 == 0:
            wb, wf = wts["even_in"][j]
            pb = norm_matmul(xf, wts["norm_mix"][i], wb, BF16).reshape(b, l, -1)
            pf = norm_matmul(xf, wts["norm_mix"][i], wf, F32).reshape(b, l, -1)
            o_ret = retention_heads(pb, cos, sin, wts["ret_norm"][j])
            lb = lb_all[j].reshape(HG_HEADS, HEAD_DIM)
            lb_tab = jnp.stack([jnp.log(lb), jnp.log1p(-lb), 1.0 - lb] + [jnp.zeros_like(lb)] * 5, axis=1)
            o_hg = hgrn_heads(pb, pf, lb_tab, wts["hgrn_norm"][j])
            wo = wts["even_out"][j]
            xf = proj_residual([o_ret.reshape(t, -1), o_hg.reshape(t, -1)],
                               [wo[:RET_HEADS * HEAD_DIM], wo[RET_HEADS * HEAD_DIM:]], xf)
        else:
            wm, wab = wts["gdn_in"][j]
            pm = norm_matmul(xf, wts["norm_mix"][i], wm, BF16).reshape(b, l, -1)
            ab = norm_matmul(xf, wts["norm_mix"][i], wab, F32, tn=128).reshape(b, l, -1)
            o = gdn_heads(pm, ab, wts["gdn_conv"][j], wts["gdn_a_log"][j], wts["gdn_dt_bias"][j],
                          wts["gdn_norm"][j])
            xf = proj_residual([o.reshape(t, -1)], [wts["gdn_out"][j]], xf)
        kv = norm_matmul(memf, wts["norm_mem"][i], wts["xa_kv"][i], BF16).reshape(b, m, 2 * d)
        xf = xattn_residual(xf.reshape(b, l, d), wts["norm_xq"][i], wts["xa_q"][i], kv,
                            wts["xa_o"][i]).reshape(t, d)
        xf = ffn_residual(xf, wts["norm_ffn"][i], wts["ffn_g"][i], wts["ffn_u"][i], wts["ffn_d"][i],
                          wts["norm_final"], final_norm=(i == DEPTH - 1))
    return xf.reshape(b, l, d)


def kernel(x_prompt, x_sample, mem_prompt, mem_sample, norm_mix, norm_xq, norm_mem, norm_ffn, norm_final, even_w_in, even_w_out, hgrn_lb_logits, ret_norm, hgrn_norm, gdn_w_in, gdn_conv, gdn_a_log, gdn_dt_bias, gdn_norm, gdn_w_out, xa_w_q, xa_w_kv, xa_w_o, ffn_w_gu, ffn_w_down):
    n_qkvg = 4 * GDN_HEADS * HEAD_DIM
    wts = {
        "norm_mix": norm_mix, "norm_xq": norm_xq, "norm_mem": norm_mem, "norm_ffn": norm_ffn,
        "norm_final": norm_final, "ret_norm": ret_norm, "hgrn_norm": hgrn_norm,
        "gdn_conv": gdn_conv, "gdn_a_log": gdn_a_log, "gdn_dt_bias": gdn_dt_bias, "gdn_norm": gdn_norm,
        "even_in": [_even_weight_split(even_w_in[j]) for j in range(even_w_in.shape[0])],
        "even_out": even_w_out.astype(BF16),
        "gdn_in": [(gdn_w_in[j][:, :n_qkvg].astype(BF16),
                    jnp.pad(gdn_w_in[j][:, n_qkvg:], ((0, 0), (0, 128 - 4 * GDN_HEADS))).astype(BF16))
                   for j in range(gdn_w_in.shape[0])],
        "gdn_out": gdn_w_out.astype(BF16),
        "xa_q": xa_w_q.astype(BF16), "xa_kv": xa_w_kv.astype(BF16), "xa_o": xa_w_o.astype(BF16),
        "ffn_g": ffn_w_gu[:, :, :D_FF].astype(BF16), "ffn_u": ffn_w_gu[:, :, D_FF:].astype(BF16),
        "ffn_d": ffn_w_down.astype(BF16),
    }
    lb_all = jnp.cumsum(jax.nn.softmax(hgrn_lb_logits.astype(F32), axis=0), axis=0)
    lb_all = lb_all - lb_all[:1]
    y_prompt = _trunk(x_prompt, mem_prompt, wts, lb_all)
    y_sample = _trunk(x_sample, mem_sample, wts, lb_all)
    return (y_prompt, y_sample)
```

```python
import functools
import math

import numpy as np
import jax
import jax.numpy as jnp
from jax import lax
from jax.experimental import pallas as pl
from jax.experimental.pallas import tpu as pltpu

F32 = jnp.float32
BF16 = jnp.bfloat16

D_MODEL = 1024
DEPTH = 4
RMS_EPS = 1e-6
ROPE_BASE = 10000.0
HEAD_DIM = 128
RET_HEADS = 4
HG_HEADS = 4
GDN_HEADS = 8
CONV_W = 5
XA_HEADS = 4
XA_DH = D_MODEL // XA_HEADS
D_FF = 2816
CHUNK = 64
NEUMANN_BLOCK = 16

VMEM_LIMIT_BYTES = 56 * 1024 * 1024


def _cparams(sem):
    return pltpu.CompilerParams(dimension_semantics=sem, vmem_limit_bytes=VMEM_LIMIT_BYTES)


def _rms(x, g):
    r = lax.rsqrt(jnp.mean(x * x, axis=-1, keepdims=True) + RMS_EPS)
    return x * r * g


def _silu(x):
    return x * (1.0 / (1.0 + jnp.exp(-x)))


def _dot(a, b):
    return jnp.dot(a.astype(BF16), b.astype(BF16), preferred_element_type=F32)


def _dot_nt(a, b):
    return lax.dot_general(a.astype(BF16), b.astype(BF16), (((1,), (1,)), ((), ())),
                           preferred_element_type=F32)


def _dot_tn(a, b):
    return lax.dot_general(a.astype(BF16), b.astype(BF16), (((0,), (0,)), ((), ())),
                           preferred_element_type=F32)


def _split3(x):
    hi = x.astype(BF16)
    r1 = x - hi.astype(F32)
    mid = r1.astype(BF16)
    lo = (r1 - mid.astype(F32)).astype(BF16)
    return hi, mid, lo


def _iota2(n, m):
    return (lax.broadcasted_iota(jnp.int32, (n, m), 0), lax.broadcasted_iota(jnp.int32, (n, m), 1))


def _norm_matmul_kernel(x_ref, g_ref, w_ref, o_ref, h_scr):
    @pl.when(pl.program_id(1) == 0)
    def _():
        h_scr[...] = _rms(x_ref[...], g_ref[...]).astype(BF16)
    o_ref[...] = jnp.dot(h_scr[...], w_ref[...], preferred_element_type=F32).astype(o_ref.dtype)


def norm_matmul(x, g, w, out_dtype, tm=1024, tn=512):
    t, d = x.shape
    n = w.shape[1]
    tm = min(tm, t)
    tn = min(tn, n)
    assert t % tm == 0 and n % tn == 0
    return pl.pallas_call(
        _norm_matmul_kernel,
        grid=(t // tm, n // tn),
        in_specs=[pl.BlockSpec((tm, d), lambda i, j: (i, 0)),
                  pl.BlockSpec((1, d), lambda i, j: (0, 0)),
                  pl.BlockSpec((d, tn), lambda i, j: (0, j))],
        out_specs=pl.BlockSpec((tm, tn), lambda i, j: (i, j)),
        out_shape=jax.ShapeDtypeStruct((t, n), out_dtype),
        scratch_shapes=[pltpu.VMEM((tm, d), BF16)],
        compiler_params=_cparams(("parallel", "arbitrary")),
        name="norm_matmul",
    )(x, g.reshape(1, d), w)


def _proj_residual_kernel(*refs, n_in):
    o_refs = refs[:n_in]
    w_refs = refs[n_in:2 * n_in]
    x_ref = refs[2 * n_in]
    out_ref = refs[2 * n_in + 1]
    acc = x_ref[...]
    for o_ref, w_ref in zip(o_refs, w_refs):
        acc = acc + jnp.dot(o_ref[...], w_ref[...], preferred_element_type=F32)
    out_ref[...] = acc


def proj_residual(os_, ws, x, tm=1024):
    t, d = x.shape
    tm = min(tm, t)
    assert t % tm == 0
    n_in = len(os_)
    in_specs = ([pl.BlockSpec((tm, o.shape[1]), lambda i: (i, 0)) for o in os_]
                + [pl.BlockSpec(w.shape, lambda i: (0, 0)) for w in ws]
                + [pl.BlockSpec((tm, d), lambda i: (i, 0))])
    return pl.pallas_call(
        functools.partial(_proj_residual_kernel, n_in=n_in),
        grid=(t // tm,),
        in_specs=in_specs,
        out_specs=pl.BlockSpec((tm, d), lambda i: (i, 0)),
        out_shape=jax.ShapeDtypeStruct((t, d), F32),
        compiler_params=_cparams(("parallel",)),
        name="proj_residual",
    )(*os_, *ws, x)


def _xattn_kernel(x_ref, g_ref, wq_ref, kv_ref, wo_ref, out_ref):
    x = x_ref[0]
    h = _rms(x, g_ref[...]).astype(BF16)
    q = jnp.dot(h, wq_ref[...], preferred_element_type=F32)
    heads = []
    for hd in range(XA_HEADS):
        qh = q[:, hd * XA_DH:(hd + 1) * XA_DH]
        kh = kv_ref[0, :, hd * XA_DH:(hd + 1) * XA_DH]
        vh = kv_ref[0, :, D_MODEL + hd * XA_DH:D_MODEL + (hd + 1) * XA_DH]
        s = _dot_nt(qh, kh) * (XA_DH ** -0.5)
        m = jnp.max(s, axis=-1, keepdims=True)
        e = jnp.exp(s - m)
        pr = e / jnp.sum(e, axis=-1, keepdims=True)
        heads.append(_dot(pr, vh).astype(BF16))
    o = jnp.concatenate(heads, axis=-1)
    out_ref[0] = x + jnp.dot(o, wo_ref[...], preferred_element_type=F32)


def xattn_residual(x, g, wq, kv, wo, tl=512):
    b, l, d = x.shape
    m = kv.shape[1]
    tl = min(tl, l)
    assert l % tl == 0
    return pl.pallas_call(
        _xattn_kernel,
        grid=(b, l // tl),
        in_specs=[pl.BlockSpec((1, tl, d), lambda i, j: (i, j, 0)),
                  pl.BlockSpec((1, d), lambda i, j: (0, 0)),
                  pl.BlockSpec((d, d), lambda i, j: (0, 0)),
                  pl.BlockSpec((1, m, 2 * d), lambda i, j: (i, 0, 0)),
                  pl.BlockSpec((d, d), lambda i, j: (0, 0))],
        out_specs=pl.BlockSpec((1, tl, d), lambda i, j: (i, j, 0)),
        out_shape=jax.ShapeDtypeStruct((b, l, d), F32),
        compiler_params=_cparams(("parallel", "parallel")),
        name="xattn_residual",
    )(x, g.reshape(1, d), wq, kv, wo)


def _ffn_kernel(x_ref, g_ref, wg_ref, wu_ref, wd_ref, gf_ref, out_ref, *, tf, final_norm):
    x = x_ref[...]
    h = _rms(x, g_ref[...]).astype(BF16)
    acc = x
    for j in range(D_FF // tf):
        gt = jnp.dot(h, wg_ref[:, j * tf:(j + 1) * tf], preferred_element_type=F32)
        up = jnp.dot(h, wu_ref[:, j * tf:(j + 1) * tf], preferred_element_type=F32)
        act = (_silu(gt) * up).astype(BF16)
        acc = acc + jnp.dot(act, wd_ref[j * tf:(j + 1) * tf, :], preferred_element_type=F32)
    if final_norm:
        acc = _rms(acc, gf_ref[...])
    out_ref[...] = acc


def ffn_residual(x, g, wg, wu, wd, gf, final_norm, tm=512, tf=256):
    t, d = x.shape
    tm = min(tm, t)
    assert t % tm == 0 and D_FF % tf == 0
    return pl.pallas_call(
        functools.partial(_ffn_kernel, tf=tf, final_norm=final_norm),
        grid=(t // tm,),
        in_specs=[pl.BlockSpec((tm, d), lambda i: (i, 0)),
                  pl.BlockSpec((1, d), lambda i: (0, 0)),
                  pl.BlockSpec((d, D_FF), lambda i: (0, 0)),
                  pl.BlockSpec((d, D_FF), lambda i: (0, 0)),
                  pl.BlockSpec((D_FF, d), lambda i: (0, 0)),
                  pl.BlockSpec((1, d), lambda i: (0, 0))],
        out_specs=pl.BlockSpec((tm, d), lambda i: (i, 0)),
        out_shape=jax.ShapeDtypeStruct((t, d), F32),
        compiler_params=_cparams(("parallel",)),
        name="ffn_residual",
    )(x, g.reshape(1, d), wg, wu, wd, gf.reshape(1, d))


def _retention_tables(c):
    lg = np.log1p(-np.exp2(-5.0 - np.arange(RET_HEADS, dtype=np.float64)))
    idx = np.arange(c, dtype=np.float64)
    dist = np.abs(idx[:, None] - idx[None, :])
    kscale = HEAD_DIM ** -0.5
    dmat = np.exp(lg[:, None, None] * dist[None]) * (1.0 + np.eye(c))[None] * kscale
    rows = np.stack([np.exp(lg[:, None] * (idx + 1.0)[None]),
                     np.exp(lg[:, None] * (c - 1.0 - idx)[None]) * kscale,
                     np.exp(lg[:, None] * (c - idx)[None]),
                     np.exp(lg[:, None] * idx[None]) * kscale], axis=1)
    rows = np.broadcast_to(rows[..., None], rows.shape + (HEAD_DIM,))
    gc = np.broadcast_to(np.exp(lg * c)[:, None, None], (RET_HEADS, 8, HEAD_DIM))
    return (jnp.asarray(dmat, F32), jnp.asarray(rows, F32), jnp.asarray(gc, F32))


def _rope(x, cos, sin):
    return x * cos + pltpu.roll(x, HEAD_DIM // 2, 1) * sin


def _retention_kernel(q_ref, k_ref, v_ref, gate_ref, cos_ref, sin_ref, dmat_ref, rows_ref, gc_ref,
                      gain_ref, out_ref, o_scr, *, n_chunks, c):
    dmat = dmat_ref[0]
    gc = gc_ref[0, 0:1, :]

    def load(i):
        sl = pl.ds(pl.multiple_of(i * c, c), c)
        q = _rope(q_ref[0, sl, :].astype(F32), cos_ref[sl, :], sin_ref[sl, :])
        k = _rope(k_ref[0, sl, :].astype(F32), cos_ref[sl, :], sin_ref[sl, :])
        return sl, q, k, v_ref[0, sl, :]

    def fwd(i, s):
        sl, q, k, v = load(i)
        a = _dot_nt(q, k) * dmat
        o_scr[sl, :] = _dot(a, v) + _dot(q * rows_ref[0, 0], s)
        return gc * s + _dot_tn(k * rows_ref[0, 1], v)

    lax.fori_loop(0, n_chunks, fwd, jnp.zeros((HEAD_DIM, HEAD_DIM), F32))

    def bwd(t, s):
        i = n_chunks - 1 - t
        sl, q, k, v = load(i)
        o = o_scr[sl, :] + _dot(q * rows_ref[0, 2], s)
        mu = jnp.mean(o, axis=-1, keepdims=True)
        oc = o - mu
        o = oc * lax.rsqrt(jnp.mean(oc * oc, axis=-1, keepdims=True) + RMS_EPS) * gain_ref[0]
        out_ref[0, sl, :] = (o * _silu(gate_ref[0, sl, :].astype(F32))).astype(out_ref.dtype)
        return gc * s + _dot_tn(k * rows_ref[0, 3], v)

    lax.fori_loop(0, n_chunks, bwd, jnp.zeros((HEAD_DIM, HEAD_DIM), F32))


def retention_heads(pb, cos, sin, gain, c=CHUNK):
    b, l, _ = pb.shape
    hh = RET_HEADS
    dmat, rows, gc = _retention_tables(c)
    col = lambda off: pl.BlockSpec((1, l, HEAD_DIM), lambda i, h: (i, 0, off + h))
    return pl.pallas_call(
        functools.partial(_retention_kernel, n_chunks=l // c, c=c),
        grid=(b, hh),
        in_specs=[col(0), col(hh), col(2 * hh), col(3 * hh),
                  pl.BlockSpec((l, HEAD_DIM), lambda i, h: (0, 0)),
                  pl.BlockSpec((l, HEAD_DIM), lambda i, h: (0, 0)),
                  pl.BlockSpec((1, c, c), lambda i, h: (h, 0, 0)),
                  pl.BlockSpec((1, 4, c, HEAD_DIM), lambda i, h: (h, 0, 0, 0)),
                  pl.BlockSpec((1, 8, HEAD_DIM), lambda i, h: (h, 0, 0)),
                  pl.BlockSpec((1, 1, HEAD_DIM), lambda i, h: (h, 0, 0))],
        out_specs=pl.BlockSpec((1, l, HEAD_DIM), lambda i, h: (i, 0, h)),
        out_shape=jax.ShapeDtypeStruct((b, l, hh * HEAD_DIM), BF16),
        scratch_shapes=[pltpu.VMEM((l, HEAD_DIM), F32)],
        compiler_params=_cparams(("parallel", "parallel")),
        name="retention_heads",
    )(pb, pb, pb, pb, cos, sin, dmat, rows, gc, gain.reshape(hh, 1, HEAD_DIM))


EXP_CLAMP = 80.0


def _hgrn_gate(z, log_lb, log1m_lb, one_m_lb):
    ls = jnp.minimum(z, 0.0) - jnp.log(1.0 + jnp.exp(-jnp.abs(z)))
    bb = log1m_lb + ls
    mx = jnp.maximum(log_lb, bb)
    lf = mx + jnp.log(1.0 + jnp.exp(-jnp.abs(log_lb - bb)))
    kk = one_m_lb * jnp.exp(ls - z)
    return lf, kk


def _hgrn_kernel(q_ref, v_ref, gate_ref, zf_ref, zb_ref, lb_ref, gain_ref, out_ref, of_scr, ob_scr, s_scr,
                 *, n_chunks, c, g):
    row, colm = _iota2(c, c)
    tri_f = jnp.where(row >= colm, 1.0, 0.0).astype(BF16)
    tri_b = jnp.where(row <= colm, 1.0, 0.0).astype(BF16)
    chains = [(hd, rev) for hd in range(g) for rev in (False, True)]
    heads = [hd for hd, _ in chains]
    revs = [rev for _, rev in chains]
    lanes = [slice(hd * HEAD_DIM, (hd + 1) * HEAD_DIM) for hd in heads]

    def step(t, _):
        idx = [(n_chunks - 1 - t) if rev else t for rev in revs]
        sls = [pl.ds(pl.multiple_of(i * c, c), c) for i in idx]
        st = [s_scr[j] for j in range(len(chains))]
        q = _lockstep(lambda sl, ln: q_ref[0, sl, ln].astype(F32), sls, lanes)
        v = _lockstep(lambda sl, ln: v_ref[0, sl, ln], sls, lanes)
        z = _lockstep(lambda sl, ln, rev: (zb_ref if rev else zf_ref)[0, sl, ln], sls, lanes, revs)
        gk = _lockstep(lambda z_, hd: _hgrn_gate(z_, lb_ref[hd, 0:1, :], lb_ref[hd, 1:2, :],
                                                 lb_ref[hd, 2:3, :]), z, heads)
        kk = [x[1] for x in gk]
        parts = _lockstep(lambda x: _split3(x[0]), gk)

        def cumsum(p3, rev):
            tri = tri_b if rev else tri_f
            return (jnp.dot(tri, p3[0], preferred_element_type=F32)
                    + jnp.dot(tri, p3[1], preferred_element_type=F32)
                    + jnp.dot(tri, p3[2], preferred_element_type=F32))
        bcum = _lockstep(cumsum, parts, revs)
        mid_row = [c // 2 if rev else c // 2 - 1 for rev in revs]
        end_row = [0 if rev else c - 1 for rev in revs]
        b_mid = _lockstep(lambda b_, r: b_[r:r + 1, :], bcum, mid_row)
        b_end = _lockstep(lambda b_, r: b_[r:r + 1, :], bcum, end_row)
        e1 = _lockstep(lambda b_, bm: jnp.exp(jnp.minimum(b_ - bm, EXP_CLAMP)), bcum, b_mid)
        e2 = _lockstep(lambda b_, bm: jnp.exp(jnp.minimum(bm - b_, EXP_CLAMP)), bcum, b_mid)
        a = _lockstep(lambda q_, e1_, kk_, e2_: _dot_nt(q_ * e1_, kk_ * e2_), q, e1, kk, e2)
        a = _lockstep(lambda a_, rev: jnp.where((row <= colm) if rev else (row >= colm), a_, 0.0), a, revs)
        av = _lockstep(_dot, a, v)
        qs = _lockstep(lambda q_, e1_, bm, st_: _dot_nt(q_ * (e1_ * jnp.exp(bm)), st_), q, e1, b_mid, st)
        vk = _lockstep(lambda v_, kk_, e2_, be, bm: _dot_tn(v_, kk_ * (e2_ * jnp.exp(be - bm))),
                       v, kk, e2, b_end, b_mid)
        for j, (hd, rev) in enumerate(chains):
            s_scr[j] = jnp.exp(b_end[j]) * st[j] + vk[j]
            (ob_scr if rev else of_scr)[hd, sls[j], :] = av[j] + qs[j]
        return 0

    for j in range(len(chains)):
        s_scr[j] = jnp.zeros((HEAD_DIM, HEAD_DIM), F32)

    lax.fori_loop(0, n_chunks, step, 0)

    def finish(i, _):
        sl = pl.ds(pl.multiple_of(i * c, c), c)
        for hd in range(g):
            ln = slice(hd * HEAD_DIM, (hd + 1) * HEAD_DIM)
            o = of_scr[hd, sl, :] + ob_scr[hd, sl, :]
            o = o * lax.rsqrt(jnp.mean(o * o, axis=-1, keepdims=True) + RMS_EPS) * gain_ref[hd]
            out_ref[0, sl, ln] = (o * _silu(gate_ref[0, sl, ln].astype(F32))).astype(out_ref.dtype)
        return 0

    lax.fori_loop(0, n_chunks, finish, 0)


HGRN_HEADS_PER_STEP = 2


def hgrn_heads(pb, pf, lb_tab, gain, c=CHUNK, g=HGRN_HEADS_PER_STEP):
    b, l, _ = pb.shape
    hh = HG_HEADS
    ng = hh // g
    gw = g * HEAD_DIM
    assert hh % g == 0
    colb = lambda grp: pl.BlockSpec((1, l, gw), lambda i, h: (i, 0, grp * ng + h))
    return pl.pallas_call(
        functools.partial(_hgrn_kernel, n_chunks=l // c, c=c, g=g),
        grid=(b, ng),
        in_specs=[colb(4), colb(5), colb(6), colb(0), colb(1),
                  pl.BlockSpec((g, 8, HEAD_DIM), lambda i, h: (h, 0, 0)),
                  pl.BlockSpec((g, 1, HEAD_DIM), lambda i, h: (h, 0, 0))],
        out_specs=pl.BlockSpec((1, l, gw), lambda i, h: (i, 0, h)),
        out_shape=jax.ShapeDtypeStruct((b, l, hh * HEAD_DIM), BF16),
        scratch_shapes=[pltpu.VMEM((g, l, HEAD_DIM), F32),
                        pltpu.VMEM((g, l, HEAD_DIM), F32),
                        pltpu.VMEM((2 * g, HEAD_DIM, HEAD_DIM), F32)],
        compiler_params=_cparams(("parallel", "parallel")),
        name="hgrn_heads",
    )(pb, pb, pb, pf, pf, lb_tab, gain.reshape(hh, 1, HEAD_DIM))


def _lockstep(f, *lists):
    return [f(*xs) for xs in zip(*lists)]


def _unit_triangular_inverse(a_list, same_blk, same_2blk, c):
    row, colm = _iota2(c, c)
    eye = jnp.where(row == colm, 1.0, 0.0)
    p = _lockstep(lambda a: jnp.where(same_blk, a, 0.0), a_list)
    x = _lockstep(lambda p_: eye - p_, p)
    n = 2
    while n < NEUMANN_BLOCK:
        p = _lockstep(lambda p_: _dot(p_, p_), p)
        x = _lockstep(lambda x_, p_: x_ + _dot(x_, p_), x, p)
        n *= 2
    for sel in (lambda a: jnp.where(same_2blk & jnp.logical_not(same_blk), a, 0.0),
                lambda a: jnp.where(same_2blk, 0.0, a)):
        off = _lockstep(sel, a_list)
        xo = _lockstep(_dot, x, off)
        x = _lockstep(lambda x_, xo_: x_ - _dot(xo_, x_), x, xo)
    return x


def _gdn_kernel(q_ref, k_ref, v_ref, gate_ref, cwq_ref, cwk_ref, cwv_ref, ab_ref, par_ref, gain_ref,
                out_ref, xs_scr, qn_scr, kn_scr, vn_scr, of_scr, ob_scr, s_scr, *, n_chunks, c, l, g):
    halo = 8
    row, colm = _iota2(c, c)
    eye_b = row == colm
    same_blk = (row // NEUMANN_BLOCK) == (colm // NEUMANN_BLOCK)
    same_2blk = (row // (2 * NEUMANN_BLOCK)) == (colm // (2 * NEUMANN_BLOCK))
    pre_u = jnp.where(row <= colm, 1.0, 0.0).astype(BF16)
    suf_u = jnp.where(row >= colm, 1.0, 0.0).astype(BF16)

    zeros_h = jnp.zeros((halo, HEAD_DIM), F32)

    def conv_pass(src_ref, cw_ref, dst_scr, mode, hd):
        lanes = slice(hd * HEAD_DIM, (hd + 1) * HEAD_DIM)
        xs_scr[0:halo, :] = zeros_h
        xs_scr[halo + l:2 * halo + l, :] = zeros_h

        def cp(i, _):
            sl = pl.ds(pl.multiple_of(i * c, c), c)
            xs_scr[pl.ds(pl.multiple_of(halo + i * c, 8), c), :] = src_ref[0, sl, lanes].astype(F32)
            return 0
        lax.fori_loop(0, n_chunks, cp, 0)

        def cv(i, _):
            win = xs_scr[pl.ds(pl.multiple_of(i * c, 8), c + 2 * halo), :]
            acc = jnp.zeros((c, HEAD_DIM), F32)
            for w in range(CONV_W):
                off = halo + w - CONV_W // 2
                acc = acc + win[off:off + c, :] * cw_ref[hd, w:w + 1, :]
            y = _silu(acc)
            if mode != "v":
                y = y * lax.rsqrt(jnp.sum(y * y, axis=-1, keepdims=True) + RMS_EPS)
            if mode == "q":
                y = y * (HEAD_DIM ** -0.5)
            dst_scr[hd, pl.ds(pl.multiple_of(i * c, c), c), :] = y
            return 0
        lax.fori_loop(0, n_chunks, cv, 0)

    for hd in range(g):
        conv_pass(q_ref, cwq_ref, qn_scr, "q", hd)
        conv_pass(k_ref, cwk_ref, kn_scr, "k", hd)
        conv_pass(v_ref, cwv_ref, vn_scr, "v", hd)

    def to_col(r):
        return jnp.sum(jnp.where(eye_b, r, 0.0), axis=1, keepdims=True)

    chains = [(hd, rev) for hd in range(g) for rev in (False, True)]

    def step(t, _):
        heads = [hd for hd, _ in chains]
        revs = [rev for _, rev in chains]
        idx = [(n_chunks - 1 - t) if rev else t for rev in revs]
        sls = [pl.ds(pl.multiple_of(i * c, c), c) for i in idx]
        s = [s_scr[j] for j in range(len(chains))]
        q = _lockstep(lambda hd, sl: qn_scr[hd, sl, :], heads, sls)
        k = _lockstep(lambda hd, sl: kn_scr[hd, sl, :], heads, sls)
        v = _lockstep(lambda hd, sl: vn_scr[hd, sl, :], heads, sls)
        raw = _lockstep(lambda hd, i: ab_ref[0, hd, i], heads, idx)

        def log_decay(raw_, hd):
            xa = raw_ + par_ref[hd, 1, :, 0:c]
            softplus = jnp.maximum(xa, 0.0) + jnp.log(1.0 + jnp.exp(-jnp.abs(xa)))
            return -par_ref[hd, 0, :, 0:c] * softplus
        la = _lockstep(log_decay, raw, heads)
        beta = _lockstep(lambda raw_: 1.0 / (1.0 + jnp.exp(-raw_)), raw)
        parts = _lockstep(_split3, la)

        def cumsum(p3, rev):
            um = suf_u if rev else pre_u
            return (jnp.dot(p3[0], um, preferred_element_type=F32)
                    + jnp.dot(p3[1], um, preferred_element_type=F32)
                    + jnp.dot(p3[2], um, preferred_element_type=F32))
        gs = _lockstep(cumsum, parts, revs)
        g_row = _lockstep(lambda gs_, rev: gs_[1:2, :] if rev else gs_[0:1, :], gs, revs)
        beta_row = _lockstep(lambda b_, rev: b_[3:4, :] if rev else b_[2:3, :], beta, revs)
        g_end = _lockstep(lambda gr, rev: gr[:, 0:1] if rev else gr[:, c - 1:c], g_row, revs)
        g_col = _lockstep(to_col, g_row)
        beta_col = _lockstep(to_col, beta_row)
        incl = [(row <= colm) if rev else (row >= colm) for rev in revs]
        strict = [(row < colm) if rev else (row > colm) for rev in revs]
        decay = _lockstep(lambda m, gc_, gr: jnp.exp(jnp.where(m, gc_ - gr, -jnp.inf)), incl, g_col, g_row)
        kb = _lockstep(lambda k_, bc: k_ * bc, k, beta_col)
        kk = _lockstep(_dot_nt, kb, k)
        a = _lockstep(lambda m, kk_, d_: jnp.where(m, kk_ * d_, 0.0), strict, kk, decay)
        tinv = _unit_triangular_inverse(a, same_blk, same_2blk, c)
        u = _lockstep(lambda t_, v_, bc: _dot(t_, v_ * bc), tinv, v, beta_col)
        w = _lockstep(lambda t_, kb_, gc_: _dot(t_, kb_ * jnp.exp(gc_)), tinv, kb, g_col)
        qkr = _lockstep(_dot_nt, q, k)
        qk = _lockstep(lambda m, qk_, d_: jnp.where(m, qk_ * d_, 0.0), incl, qkr, decay)
        ws = _lockstep(_dot, w, s)
        v_new = _lockstep(lambda u_, ws_: u_ - ws_, u, ws)
        qs = _lockstep(lambda q_, gc_, s_: _dot(q_ * jnp.exp(gc_), s_), q, g_col, s)
        o = _lockstep(lambda qs_, qk_, vn_: qs_ + _dot(qk_, vn_), qs, qk, v_new)
        kv = _lockstep(lambda k_, ge, gc_, vn_: _dot_tn(k_ * jnp.exp(ge - gc_), vn_), k, g_end, g_col, v_new)
        s_new = _lockstep(lambda ge, s_, kv_: jnp.exp(ge) * s_ + kv_, g_end, s, kv)
        for j, (hd, rev) in enumerate(chains):
            s_scr[j] = s_new[j]
            (ob_scr if rev else of_scr)[hd, sls[j], :] = o[j]
        return 0

    for j in range(len(chains)):
        s_scr[j] = jnp.zeros((HEAD_DIM, HEAD_DIM), F32)

    lax.fori_loop(0, n_chunks, step, 0)

    def finish(i, _):
        sl = pl.ds(pl.multiple_of(i * c, c), c)
        for hd in range(g):
            lanes = slice(hd * HEAD_DIM, (hd + 1) * HEAD_DIM)
            o = of_scr[hd, sl, :] + ob_scr[hd, sl, :]
            o = o * lax.rsqrt(jnp.mean(o * o, axis=-1, keepdims=True) + RMS_EPS) * gain_ref[...]
            out_ref[0, sl, lanes] = (o * _silu(gate_ref[0, sl, lanes].astype(F32))).astype(out_ref.dtype)
        return 0

    lax.fori_loop(0, n_chunks, finish, 0)


GDN_HEADS_PER_STEP = 2


def gdn_heads(pm, ab, conv_w, a_log, dt_bias, gain, c=CHUNK, g=GDN_HEADS_PER_STEP):
    b, l, _ = pm.shape
    hh = GDN_HEADS
    n = l // c
    ng = hh // g
    gw = g * HEAD_DIM
    assert c == 4 * NEUMANN_BLOCK and hh % g == 0
    abr = ab[:, :, :4 * hh].reshape(b, n, c, 4, hh)
    abr = jnp.transpose(abr, (0, 4, 1, 3, 2))
    abr = jnp.concatenate([abr, jnp.zeros_like(abr)], axis=3)
    pa = jnp.zeros((hh, 8), F32).at[:, 0:2].set(jnp.exp(a_log.astype(F32)).T)
    pd = jnp.zeros((hh, 8), F32).at[:, 0:2].set(dt_bias.astype(F32).T)
    par = jnp.broadcast_to(jnp.stack([pa, pd], axis=1)[..., None], (hh, 2, 8, HEAD_DIM))
    cw = conv_w.astype(F32).reshape(CONV_W, 3 * hh, HEAD_DIM).transpose(1, 0, 2)
    cw = jnp.concatenate([cw, jnp.zeros((3 * hh, 8 - CONV_W, HEAD_DIM), F32)], axis=1)
    col = lambda off: pl.BlockSpec((1, l, gw), lambda i, h: (i, 0, off + h))
    cws = lambda off: pl.BlockSpec((g, 8, HEAD_DIM), lambda i, h: (off + h, 0, 0))
    return pl.pallas_call(
        functools.partial(_gdn_kernel, n_chunks=n, c=c, l=l, g=g),
        grid=(b, ng),
        in_specs=[col(0), col(ng), col(2 * ng), col(3 * ng), cws(0), cws(ng), cws(2 * ng),
                  pl.BlockSpec((1, g, n, 8, c), lambda i, h: (i, h, 0, 0, 0)),
                  pl.BlockSpec((g, 2, 8, HEAD_DIM), lambda i, h: (h, 0, 0, 0)),
                  pl.BlockSpec((1, HEAD_DIM), lambda i, h: (0, 0))],
        out_specs=pl.BlockSpec((1, l, gw), lambda i, h: (i, 0, h)),
        out_shape=jax.ShapeDtypeStruct((b, l, hh * HEAD_DIM), BF16),
        scratch_shapes=[pltpu.VMEM((l + 16, HEAD_DIM), F32),
                        pltpu.VMEM((g, l, HEAD_DIM), F32),
                        pltpu.VMEM((g, l, HEAD_DIM), F32),
                        pltpu.VMEM((g, l, HEAD_DIM), F32),
                        pltpu.VMEM((g, l, HEAD_DIM), F32),
                        pltpu.VMEM((g, l, HEAD_DIM), F32),
                        pltpu.VMEM((2 * g, HEAD_DIM, HEAD_DIM), F32)],
        compiler_params=_cparams(("parallel", "parallel")),
        name="gdn_heads",
    )(pm, pm, pm, pm, cw, cw, cw, abr, par, gain.reshape(1, HEAD_DIM))


def _rope_tables(l):
    half = HEAD_DIM // 2
    inv = 1.0 / (ROPE_BASE ** (jnp.arange(half, dtype=F32) / half))
    ang = jnp.arange(l, dtype=F32)[:, None] * inv[None, :]
    cos, sin = jnp.cos(ang), jnp.sin(ang)
    return jnp.concatenate([cos, cos], axis=-1), jnp.concatenate([-sin, sin], axis=-1)


def _even_weight_split(w_in):
    g = 512
    rq, rk, rv, rg, hq, hff, hfb, hi, hgt = [w_in[:, i * g:(i + 1) * g] for i in range(9)]
    return (jnp.concatenate([rq, rk, rv, rg, hq, hi, hgt], axis=1).astype(BF16),
            jnp.concatenate([hff, hfb], axis=1).astype(BF16))


def _trunk(x, mem, wts, lb_all):
    b, l, d = x.shape
    t = b * l
    m = mem.shape[1]
    cos, sin = _rope_tables(l)
    xf = x.reshape(t, d)
    memf = mem.reshape(b * m, d)
    for i in range(DEPTH):
        j = i // 2
        if i % 2 == 0:
            wb, wf = wts["even_in"][j]
            pb = norm_matmul(xf, wts["norm_mix"][i], wb, BF16).reshape(b, l, -1)
            pf = norm_matmul(xf, wts["norm_mix"][i], wf, F32).reshape(b, l, -1)
            o_ret = retention_heads(pb, cos, sin, wts["ret_norm"][j])
            lb = lb_all[j].reshape(HG_HEADS, HEAD_DIM)
            lb_tab = jnp.stack([jnp.log(lb), jnp.log1p(-lb), 1.0 - lb] + [jnp.zeros_like(lb)] * 5, axis=1)
            o_hg = hgrn_heads(pb, pf, lb_tab, wts["hgrn_norm"][j])
            wo = wts["even_out"][j]
            xf = proj_residual([o_ret.reshape(t, -1), o_hg.reshape(t, -1)],
                               [wo[:RET_HEADS * HEAD_DIM], wo[RET_HEADS * HEAD_DIM:]], xf)
        else:
            wm, wab = wts["gdn_in"][j]
            pm = norm_matmul(xf, wts["norm_mix"][i], wm, BF16).reshape(b, l, -1)
            ab = norm_matmul(xf, wts["norm_mix"][i], wab, F32, tn=128).reshape(b, l, -1)
            o = gdn_heads(pm, ab, wts["gdn_conv"][j], wts["gdn_a_log"][j], wts["gdn_dt_bias"][j],
                          wts["gdn_norm"][j])
            xf = proj_residual([o.reshape(t, -1)], [wts["gdn_out"][j]], xf)
        kv = norm_matmul(memf, wts["norm_mem"][i], wts["xa_kv"][i], BF16).reshape(b, m, 2 * d)
        xf = xattn_residual(xf.reshape(b, l, d), wts["norm_xq"][i], wts["xa_q"][i], kv,
                            wts["xa_o"][i]).reshape(t, d)
        xf = ffn_residual(xf, wts["norm_ffn"][i], wts["ffn_g"][i], wts["ffn_u"][i], wts["ffn_d"][i],
                          wts["norm_final"], final_norm=(i == DEPTH - 1))
    return xf.reshape(b, l, d)


def kernel(x_prompt, x_sample, mem_prompt, mem_sample, norm_mix, norm_xq, norm_mem, norm_ffn, norm_final, even_w_in, even_w_out, hgrn_lb_logits, ret_norm, hgrn_norm, gdn_w_in, gdn_conv, gdn_a_log, gdn_dt_bias, gdn_norm, gdn_w_out, xa_w_q, xa_w_kv, xa_w_o, ffn_w_gu, ffn_w_down):
    n_qkvg = 4 * GDN_HEADS * HEAD_DIM
    wts = {
        "norm_mix": norm_mix, "norm_xq": norm_xq, "norm_mem": norm_mem, "norm_ffn": norm_ffn,
        "norm_final": norm_final, "ret_norm": ret_norm, "hgrn_norm": hgrn_norm,
        "gdn_conv": gdn_conv, "gdn_a_log": gdn_a_log, "gdn_dt_bias": gdn_dt_bias, "gdn_norm": gdn_norm,
        "even_in": [_even_weight_split(even_w_in[j]) for j in range(even_w_in.shape[0])],
        "even_out": even_w_out.astype(BF16),
        "gdn_in": [(gdn_w_in[j][:, :n_qkvg].astype(BF16),
                    jnp.pad(gdn_w_in[j][:, n_qkvg:], ((0, 0), (0, 128 - 4 * GDN_HEADS))).astype(BF16))
                   for j in range(gdn_w_in.shape[0])],
        "gdn_out": gdn_w_out.astype(BF16),
        "xa_q": xa_w_q.astype(BF16), "xa_kv": xa_w_kv.astype(BF16), "xa_o": xa_w_o.astype(BF16),
        "ffn_g": ffn_w_gu[:, :, :D_FF].astype(BF16), "ffn_u": ffn_w_gu[:, :, D_FF:].astype(BF16),
        "ffn_d": ffn_w_down.astype(BF16),
    }
    lb_all = jnp.cumsum(jax.nn.softmax(hgrn_lb_logits.astype(F32), axis=0), axis=0)
    lb_all = lb_all - lb_all[:1]
    y_prompt = _trunk(x_prompt, mem_prompt, wts, lb_all)
    y_sample = _trunk(x_sample, mem_sample, wts, lb_all)
    return (y_prompt, y_sample)
```
